```python
import math
import jax, jax.numpy as jnp
from jax import lax
import numpy as np

D_MODEL = 1024
BATCH = 1
SEQ = 16384
DEPTH = 2
DEC_BATCH = 8
DEC_SEQ = 4096
PAST_LEN = 128

CONV_W = D_MODEL
CONV_K = 3
RET_HEADS = 4
RET_DK = 256
RET_DV = 512
RET_QK = RET_HEADS * RET_DK
RET_V = RET_HEADS * RET_DV
CHUNK = 128
D_FF = 4 * D_MODEL
N_BRANCH = 2
IN_COLS = 3 * CONV_W + 2 * RET_QK + 2 * RET_V + N_BRANCH * D_MODEL
NORM_EPS = 1e-6
GN_EPS = 1e-5
ROPE_BASE = 10000.0

kernel_name = "hybrid_conv_retention_encoder"


def _rmsnorm(x, g):
    xf = x.astype(jnp.float32)
    xf = xf * lax.rsqrt(jnp.mean(xf * xf, axis=-1, keepdims=True) + NORM_EPS)
    return xf.astype(x.dtype) * g


def _rotary(x, cos, sin):
    x1, x2 = jnp.split(x, 2, axis=-1)
    return jnp.concatenate([x1 * cos - x2 * sin, x1 * sin + x2 * cos], axis=-1)


def _retention_dir(q, k, v, log_gamma, include_diag):
    b, h, s, dk = q.shape
    dv = v.shape[-1]
    n = s // CHUNK
    dt = q.dtype
    qc = q.reshape(b, h, n, CHUNK, dk)
    kc = k.reshape(b, h, n, CHUNK, dk)
    vc = v.reshape(b, h, n, CHUNK, dv)
    idx = jnp.arange(CHUNK, dtype=jnp.float32)
    diff = idx[:, None] - idx[None, :]
    mask = diff >= 0 if include_diag else diff > 0
    lg = log_gamma[:, None, None]
    decay_in = (jnp.exp(jnp.where(mask, diff, 0.0)[None] * lg) * mask[None]).astype(dt)
    scores = jnp.einsum('bhncd,bhnjd->bhncj', qc, kc) * decay_in[None, :, None]
    inner = jnp.einsum('bhncj,bhnje->bhnce', scores, vc)
    q_dec = jnp.exp((idx[None, :] + 1.0) * log_gamma[:, None]).astype(dt)
    k_dec = jnp.exp((CHUNK - 1.0 - idx[None, :]) * log_gamma[:, None]).astype(dt)
    c_dec = jnp.exp(CHUNK * log_gamma).astype(dt)
    qs = jnp.moveaxis(qc * q_dec[None, :, None, :, None], 2, 0)
    ks = jnp.moveaxis(kc * k_dec[None, :, None, :, None], 2, 0)
    vs = jnp.moveaxis(vc, 2, 0)

    def step(state, xs):
        qi, ki, vi = xs
        out = jnp.einsum('bhcd,bhde->bhce', qi, state)
        state = state * c_dec[None, :, None, None] + jnp.einsum('bhcd,bhce->bhde', ki, vi)
        return state, out

    s0 = jnp.zeros((b, h, dk, dv), dtype=v.dtype)
    _, cross = lax.scan(step, s0, (qs, ks, vs))
    cross = jnp.moveaxis(cross, 0, 2)
    return (inner + cross).reshape(b, h, s, dv)


def _layer(x, norm_mix, w_in, conv_w, w_conv_out, ret_decay_fwd, ret_decay_bwd,
           ret_gn_w, ret_gn_b, w_ret_out, gate_b, w_mix_out, norm_mlp, w_mlp_in, w_mlp_out):
    b, s, _ = x.shape
    h = _rmsnorm(x, norm_mix)
    proj = h @ w_in
    splits = np.cumsum([CONV_W, CONV_W, CONV_W, RET_QK, RET_QK, RET_V, RET_V]).tolist()
    cb, cc, cx, q, k, v, g_ret, gates = jnp.split(proj, splits, axis=-1)

    z = cc * cx
    zp = jnp.pad(z, ((0, 0), (1, 1), (0, 0)))
    zc = conv_w[0] * zp[:, :-2] + conv_w[1] * zp[:, 1:-1] + conv_w[2] * zp[:, 2:]
    y_conv = (cb * zc) @ w_conv_out

    pos = jnp.arange(s, dtype=jnp.float32)
    theta = 1.0 / (ROPE_BASE ** jnp.linspace(0.0, 1.0, RET_DK // 2, dtype=jnp.float32))
    ang = pos[:, None] * theta[None, :]
    cos = jnp.cos(ang)[:, None, :].astype(x.dtype)
    sin = jnp.sin(ang)[:, None, :].astype(x.dtype)
    q = _rotary(q.reshape(b, s, RET_HEADS, RET_DK), cos, sin)
    k = _rotary(k.reshape(b, s, RET_HEADS, RET_DK), cos, sin) * (RET_DK ** -0.5)
    v = v.reshape(b, s, RET_HEADS, RET_DV)
    q, k, v = (jnp.transpose(t, (0, 2, 1, 3)) for t in (q, k, v))
    lg_f = jnp.log1p(-jnp.exp(ret_decay_fwd.astype(jnp.float32)))
    lg_b = jnp.log1p(-jnp.exp(ret_decay_bwd.astype(jnp.float32)))
    o_f = _retention_dir(q, k, v, lg_f, True)
    o_b = jnp.flip(_retention_dir(jnp.flip(q, 2), jnp.flip(k, 2), jnp.flip(v, 2), lg_b, False), 2)
    o = jnp.transpose(o_f + o_b, (0, 2, 1, 3))
    of = o.astype(jnp.float32)
    mu = jnp.mean(of, axis=-1, keepdims=True)
    var = jnp.mean(jnp.square(of - mu), axis=-1, keepdims=True)
    o = ((of - mu) * lax.rsqrt(var + GN_EPS)).astype(x.dtype).reshape(b, s, RET_V)
    o = o * ret_gn_w + ret_gn_b
    y_ret = (jax.nn.silu(g_ret) * o) @ w_ret_out

    ga, gb = jnp.split(jax.nn.sigmoid(gates + gate_b), 2, axis=-1)
    x = x + (ga * y_conv + gb * y_ret) @ w_mix_out

    h2 = _rmsnorm(x, norm_mlp)
    x = x + jnp.square(jax.nn.relu(h2 @ w_mlp_in)) @ w_mlp_out
    return x


def _trunk(x, norm_mix, w_in, conv_w, w_conv_out, ret_decay_fwd, ret_decay_bwd,
           ret_gn_w, ret_gn_b, w_ret_out, gate_b, w_mix_out, norm_mlp, w_mlp_in, w_mlp_out, norm_final):
    for l in range(DEPTH):
        x = _layer(x, norm_mix[l], w_in[l], conv_w[l], w_conv_out[l], ret_decay_fwd[l], ret_decay_bwd[l],
                   ret_gn_w[l], ret_gn_b[l], w_ret_out[l], gate_b[l], w_mix_out[l],
                   norm_mlp[l], w_mlp_in[l], w_mlp_out[l])
    return _rmsnorm(x, norm_final)


def setup_inputs(seed: int = 0) -> dict:
    key = jax.random.key(seed)
    ks = jax.random.split(key, 20)
    f32 = jnp.float32
    nrm = lambda k, shape, scale: jax.random.normal(k, shape, f32) * scale
    base_decay = math.log(2.0) * (-5.0 - jnp.arange(RET_HEADS, dtype=f32))
    return {
        "x_prompt": nrm(ks[0], (BATCH, SEQ, D_MODEL), 1.0),
        "x_sample": nrm(ks[1], (DEC_BATCH, DEC_SEQ, D_MODEL), 1.0),
        "norm_mix": 1.0 + nrm(ks[2], (DEPTH, D_MODEL), 0.02),
        "w_in": nrm(ks[3], (DEPTH, D_MODEL, IN_COLS), D_MODEL ** -0.5),
        "conv_w": nrm(ks[4], (DEPTH, CONV_K, CONV_W), CONV_K ** -0.5),
        "w_conv_out": nrm(ks[5], (DEPTH, CONV_W, D_MODEL), CONV_W ** -0.5),
        "ret_decay_fwd": base_decay[None] + nrm(ks[6], (DEPTH, RET_HEADS), 0.1),
        "ret_decay_bwd": base_decay[None] + nrm(ks[7], (DEPTH, RET_HEADS), 0.1),
        "ret_gn_w": 1.0 + nrm(ks[8], (DEPTH, RET_V), 0.02),
        "ret_gn_b": nrm(ks[9], (DEPTH, RET_V), 0.02),
        "w_ret_out": nrm(ks[10], (DEPTH, RET_V, D_MODEL), RET_V ** -0.5),
        "gate_b": nrm(ks[11], (DEPTH, N_BRANCH * D_MODEL), 0.02),
        "w_mix_out": nrm(ks[12], (DEPTH, D_MODEL, D_MODEL), D_MODEL ** -0.5),
        "norm_mlp": 1.0 + nrm(ks[13], (DEPTH, D_MODEL), 0.02),
        "w_mlp_in": nrm(ks[14], (DEPTH, D_MODEL, D_FF), D_MODEL ** -0.5),
        "w_mlp_out": nrm(ks[15], (DEPTH, D_FF, D_MODEL), D_FF ** -0.5),
        "norm_final": 1.0 + nrm(ks[16], (D_MODEL,), 0.02),
    }


def reference(x_prompt, x_sample, norm_mix, w_in, conv_w, w_conv_out, ret_decay_fwd, ret_decay_bwd,
              ret_gn_w, ret_gn_b, w_ret_out, gate_b, w_mix_out, norm_mlp, w_mlp_in, w_mlp_out, norm_final):
    y_prompt = _trunk(x_prompt, norm_mix, w_in, conv_w, w_conv_out, ret_decay_fwd, ret_decay_bwd,
                      ret_gn_w, ret_gn_b, w_ret_out, gate_b, w_mix_out, norm_mlp, w_mlp_in, w_mlp_out, norm_final)
    y_sample = _trunk(x_sample, norm_mix, w_in, conv_w, w_conv_out, ret_decay_fwd, ret_decay_bwd,
                      ret_gn_w, ret_gn_b, w_ret_out, gate_b, w_mix_out, norm_mlp, w_mlp_in, w_mlp_out, norm_final)
    return (y_prompt, y_sample)
```

```python
import functools
import math

import jax
import jax.numpy as jnp
from jax import lax
from jax.experimental import pallas as pl
from jax.experimental.pallas import tpu as pltpu

D_MODEL = 1024
CONV_W = D_MODEL
RET_HEADS = 4
RET_DK = 256
RET_DV = 512
RET_QK = RET_HEADS * RET_DK
RET_V = RET_HEADS * RET_DV
CHUNK = 128
D_FF = 4 * D_MODEL
NORM_EPS = 1e-6
GN_EPS = 1e-5
ROPE_BASE = 10000.0

HALO = 16
TM_QKV = 256
TM_MIX = 256
TM_MLP = 512
VMEM_LIMIT_BYTES = 58 * 1024 * 1024

F32 = jnp.float32
BF16 = jnp.bfloat16


def _rmsnorm(x, g):
    ms = jnp.mean(x * x, axis=-1, keepdims=True)
    return x * lax.rsqrt(ms + NORM_EPS) * g


def _row_index(shape):
    return lax.broadcasted_iota(jnp.int32, shape, 0).astype(F32)


def _log_gamma(decay_scalar):
    d = jnp.full((8, 128), decay_scalar, dtype=F32)
    return jnp.log1p(-jnp.exp(d))


def _tile_rows(t, rows):
    return jnp.broadcast_to(t[0:1, :], (rows, 128))


def _dot(a, b):
    return jnp.dot(a, b, preferred_element_type=F32)


def _dot_tn(a, b):
    return lax.dot_general(a, b, (((0,), (0,)), ((), ())), preferred_element_type=F32)


def _dot_nt(a, b):
    return lax.dot_general(a, b, (((1,), (1,)), ((), ())), preferred_element_type=F32)


def _qkv_bwd_kernel(dec_ref, x_ref, g_ref, w_ref, cos_ref, sin_ref,
                    q_ref, k_ref, v_ref, cb_ref, proj_sc, state_sc, *, tm):
    @pl.when(pl.program_id(1) == 0)
    def _():
        state_sc[...] = jnp.zeros_like(state_sc)

    h = _rmsnorm(x_ref[0], g_ref[...]).astype(BF16)
    proj_sc[...] = _dot(h, w_ref[...])

    row = _row_index((CHUNK, 128))
    qdec, kdec, cdec = [], [], []
    for hd in range(RET_HEADS):
        lg = _log_gamma(dec_ref[hd])
        lg_rows = _tile_rows(lg, CHUNK)
        qdec.append(jnp.exp((CHUNK - row) * lg_rows))
        kdec.append(jnp.exp(row * lg_rows))
        cdec.append(jnp.exp(CHUNK * lg)[0:1, 0:1])

    k_scale = RET_DK ** -0.5
    for c in reversed(range(tm // CHUNK)):
        rows = pl.ds(c * CHUNK, CHUNK)
        cos = cos_ref[rows, :]
        sin = sin_ref[rows, :]
        for hd in range(RET_HEADS):
            q0 = hd * RET_DK
            k0 = RET_QK + hd * RET_DK
            v0 = 2 * RET_QK + hd * RET_DV
            q1 = proj_sc[rows, q0:q0 + 128]
            q2 = proj_sc[rows, q0 + 128:q0 + 256]
            k1 = proj_sc[rows, k0:k0 + 128]
            k2 = proj_sc[rows, k0 + 128:k0 + 256]
            qr = jnp.concatenate([q1 * cos - q2 * sin, q1 * sin + q2 * cos], axis=-1)
            kr = jnp.concatenate([k1 * cos - k2 * sin, k1 * sin + k2 * cos], axis=-1) * k_scale
            vb = proj_sc[rows, v0:v0 + RET_DV].astype(BF16)
            q_ref[0, rows, q0:q0 + RET_DK] = qr.astype(BF16)
            k_ref[0, rows, q0:q0 + RET_DK] = kr.astype(BF16)
            v_ref[0, rows, hd * RET_DV:(hd + 1) * RET_DV] = vb

            qd = (qr * jnp.concatenate([qdec[hd], qdec[hd]], axis=-1)).astype(BF16)
            kd = (kr * jnp.concatenate([kdec[hd], kdec[hd]], axis=-1)).astype(BF16)
            state = state_sc[hd]
            cross = _dot(qd, state.astype(BF16))
            cb_ref[0, rows, hd * RET_DV:(hd + 1) * RET_DV] = cross.astype(BF16)
            state_sc[hd] = state * cdec[hd] + _dot_tn(kd, vb)


def _qkv_bwd_call(x, dec_bwd, g, w_qkv, cos, sin, *, tm):
    b, s, _ = x.shape
    nt = s // tm
    rev = lambda j: nt - 1 - j
    kern = functools.partial(_qkv_bwd_kernel, tm=tm)
    const = lambda shape: pl.BlockSpec(shape, lambda i, j: (0,) * len(shape), pipeline_mode=pl.Buffered(1))
    return pl.pallas_call(
        kern,
        grid=(b, nt),
        in_specs=[
            pl.BlockSpec(memory_space=pltpu.SMEM),
            pl.BlockSpec((1, tm, D_MODEL), lambda i, j: (i, rev(j), 0)),
            const((1, D_MODEL)),
            const((D_MODEL, 2 * RET_QK + RET_V)),
            pl.BlockSpec((tm, RET_DK // 2), lambda i, j: (rev(j), 0)),
            pl.BlockSpec((tm, RET_DK // 2), lambda i, j: (rev(j), 0)),
        ],
        out_specs=[
            pl.BlockSpec((1, tm, RET_QK), lambda i, j: (i, rev(j), 0)),
            pl.BlockSpec((1, tm, RET_QK), lambda i, j: (i, rev(j), 0)),
            pl.BlockSpec((1, tm, RET_V), lambda i, j: (i, rev(j), 0)),
            pl.BlockSpec((1, tm, RET_V), lambda i, j: (i, rev(j), 0)),
        ],
        out_shape=[
            jax.ShapeDtypeStruct((b, s, RET_QK), BF16),
            jax.ShapeDtypeStruct((b, s, RET_QK), BF16),
            jax.ShapeDtypeStruct((b, s, RET_V), BF16),
            jax.ShapeDtypeStruct((b, s, RET_V), BF16),
        ],
        scratch_shapes=[
            pltpu.VMEM((tm, 2 * RET_QK + RET_V), F32),
            pltpu.VMEM((RET_HEADS, RET_DK, RET_DV), F32),
        ],
        compiler_params=pltpu.CompilerParams(
            dimension_semantics=("arbitrary", "arbitrary"), vmem_limit_bytes=VMEM_LIMIT_BYTES),
        name="qkv_bwd",
    )(dec_bwd, x, g, w_qkv, cos, sin)


def _mix_kernel(decf_ref, decb_ref, x_ref, xp_ref, xn_ref, q_ref, k_ref, v_ref, cb_ref,
                g_ref, w_ref, cw_ref, wco_ref, gnw_ref, gnb_ref, wro_ref, gb_ref, wmo_ref,
                o_ref, z_sc, ret_sc, state_sc, *, tm):
    j = pl.program_id(1)
    last = pl.num_programs(1) - 1

    @pl.when(j == 0)
    def _():
        state_sc[...] = jnp.zeros_like(state_sc)

    x = x_ref[0]
    g = g_ref[...]
    h = _rmsnorm(x, g).astype(BF16)
    hp = _rmsnorm(xp_ref[0], g).astype(BF16)
    hn = _rmsnorm(xn_ref[0], g).astype(BF16)
    h_ext = jnp.concatenate([hp, h, hn], axis=0)

    z = _dot(h_ext, w_ref[:, CONV_W:2 * CONV_W]) * _dot(h_ext, w_ref[:, 2 * CONV_W:3 * CONV_W])
    r = lax.broadcasted_iota(jnp.int32, (tm + 2 * HALO, 1), 0)
    lo = jnp.where(j == 0, HALO, 0)
    hi = jnp.where(j == last, tm + HALO, tm + 2 * HALO)
    z_sc[...] = jnp.where((r >= lo) & (r < hi), z, 0.0)
    zc = (cw_ref[0:1, :] * z_sc[pl.ds(HALO - 1, tm), :]
          + cw_ref[1:2, :] * z_sc[pl.ds(HALO, tm), :]
          + cw_ref[2:3, :] * z_sc[pl.ds(HALO + 1, tm), :])
    u = (_dot(h, w_ref[:, 0:CONV_W]) * zc).astype(BF16)
    y_conv = _dot(u, wco_ref[...])

    row = _row_index((CHUNK, CHUNK))
    col = lax.broadcasted_iota(jnp.int32, (CHUNK, CHUNK), 1).astype(F32)
    diff = row - col
    dmask, qdec, kdec, cdec = [], [], [], []
    for hd in range(RET_HEADS):
        lgf = _log_gamma(decf_ref[hd])
        lgb = _log_gamma(decb_ref[hd])
        lgf_rows = _tile_rows(lgf, CHUNK)
        lgb_rows = _tile_rows(lgb, CHUNK)
        dmask.append(jnp.where(diff >= 0, jnp.exp(jnp.maximum(diff, 0.0) * lgf_rows),
                               jnp.exp(jnp.maximum(-diff, 0.0) * lgb_rows)))
        qdec.append(jnp.exp((row + 1.0) * lgf_rows))
        kdec.append(jnp.exp((CHUNK - 1.0 - row) * lgf_rows))
        cdec.append(jnp.exp(CHUNK * lgf)[0:1, 0:1])

    for c in range(tm // CHUNK):
        rows = pl.ds(c * CHUNK, CHUNK)
        for hd in range(RET_HEADS):
            qb = q_ref[0, rows, hd * RET_DK:(hd + 1) * RET_DK]
            kb = k_ref[0, rows, hd * RET_DK:(hd + 1) * RET_DK]
            vb = v_ref[0, rows, hd * RET_DV:(hd + 1) * RET_DV]
            p = (_dot_nt(qb, kb) * dmask[hd]).astype(BF16)
            qd = (qb.astype(F32) * jnp.concatenate([qdec[hd], qdec[hd]], axis=-1)).astype(BF16)
            kd = (kb.astype(F32) * jnp.concatenate([kdec[hd], kdec[hd]], axis=-1)).astype(BF16)
            state = state_sc[hd]
            o = (_dot(p, vb) + _dot(qd, state.astype(BF16))
                 + cb_ref[0, rows, hd * RET_DV:(hd + 1) * RET_DV].astype(F32))
            state_sc[hd] = state * cdec[hd] + _dot_tn(kd, vb)
            mu = jnp.mean(o, axis=-1, keepdims=True)
            d = o - mu
            var = jnp.mean(d * d, axis=-1, keepdims=True)
            ret_sc[rows, hd * RET_DV:(hd + 1) * RET_DV] = d * lax.rsqrt(var + GN_EPS)

    o_n = ret_sc[...] * gnw_ref[...] + gnb_ref[...]
    gr = _dot(h, w_ref[:, 3 * CONV_W:3 * CONV_W + RET_V])
    y_ret = _dot((gr * jax.nn.sigmoid(gr) * o_n).astype(BF16), wro_ref[...])

    gates = jax.nn.sigmoid(_dot(h, w_ref[:, 3 * CONV_W + RET_V:]) + gb_ref[...])
    merged = (gates[:, :D_MODEL] * y_conv + gates[:, D_MODEL:] * y_ret).astype(BF16)
    o_ref[0] = x + _dot(merged, wmo_ref[...])


def _mix_call(x, q, k, v, cb, dec_fwd, dec_bwd, g, w_rest, conv_w, w_conv_out, gn_w, gn_b,
              w_ret_out, gate_b, w_mix_out, *, tm):
    b, s, _ = x.shape
    nt = s // tm
    hb = tm // HALO
    n_hb = s // HALO
    kern = functools.partial(_mix_kernel, tm=tm)
    const = lambda shape: pl.BlockSpec(shape, lambda i, j: (0,) * len(shape), pipeline_mode=pl.Buffered(1))
    tile = lambda width: pl.BlockSpec((1, tm, width), lambda i, j: (i, j, 0))
    smem = pl.BlockSpec(memory_space=pltpu.SMEM)
    return pl.pallas_call(
        kern,
        grid=(b, nt),
        in_specs=[
            smem, smem,
            tile(D_MODEL),
            pl.BlockSpec((1, HALO, D_MODEL), lambda i, j: (i, jnp.maximum(j * hb - 1, 0), 0)),
            pl.BlockSpec((1, HALO, D_MODEL), lambda i, j: (i, jnp.minimum((j + 1) * hb, n_hb - 1), 0)),
            tile(RET_QK), tile(RET_QK), tile(RET_V), tile(RET_V),
            const((1, D_MODEL)),
            const((D_MODEL, 3 * CONV_W + RET_V + 2 * D_MODEL)),
            const((3, CONV_W)),
            const((CONV_W, D_MODEL)),
            const((1, RET_V)), const((1, RET_V)),
            const((RET_V, D_MODEL)),
            const((1, 2 * D_MODEL)),
            const((D_MODEL, D_MODEL)),
        ],
        out_specs=tile(D_MODEL),
        out_shape=jax.ShapeDtypeStruct((b, s, D_MODEL), F32),
        scratch_shapes=[
            pltpu.VMEM((tm + 2 * HALO, CONV_W), F32),
            pltpu.VMEM((tm, RET_V), F32),
            pltpu.VMEM((RET_HEADS, RET_DK, RET_DV), F32),
        ],
        compiler_params=pltpu.CompilerParams(
            dimension_semantics=("arbitrary", "arbitrary"), vmem_limit_bytes=VMEM_LIMIT_BYTES),
        name="mix",
    )(dec_fwd, dec_bwd, x, x, x, q, k, v, cb, g, w_rest, conv_w, w_conv_out, gn_w, gn_b,
      w_ret_out, gate_b, w_mix_out)


def _mlp_kernel(x_ref, g_ref, w1_ref, w2_ref, gf_ref, o_ref, *, final_norm):
    x = x_ref[...]
    h = _rmsnorm(x, g_ref[...]).astype(BF16)
    acc = x
    for c in range(D_FF // D_MODEL):
        cols = slice(c * D_MODEL, (c + 1) * D_MODEL)
        a = jnp.maximum(_dot(h, w1_ref[:, cols]), 0.0)
        acc = acc + _dot((a * a).astype(BF16), w2_ref[cols, :])
    if final_norm:
        acc = _rmsnorm(acc, gf_ref[...])
    o_ref[...] = acc


def _mlp_call(x2d, g, w1, w2, g_final, *, tm, final_norm):
    t = x2d.shape[0]
    kern = functools.partial(_mlp_kernel, final_norm=final_norm)
    const = lambda shape: pl.BlockSpec(shape, lambda i: (0,) * len(shape), pipeline_mode=pl.Buffered(1))
    return pl.pallas_call(
        kern,
        grid=(t // tm,),
        in_specs=[
            pl.BlockSpec((tm, D_MODEL), lambda i: (i, 0)),
            const((1, D_MODEL)),
            const((D_MODEL, D_FF)),
            const((D_FF, D_MODEL)),
            const((1, D_MODEL)),
        ],
        out_specs=pl.BlockSpec((tm, D_MODEL), lambda i: (i, 0)),
        out_shape=jax.ShapeDtypeStruct((t, D_MODEL), F32),
        compiler_params=pltpu.CompilerParams(
            dimension_semantics=("arbitrary",), vmem_limit_bytes=VMEM_LIMIT_BYTES),
        name="mlp",
    )(x2d, g, w1, w2, g_final)


def _rope_tables(s):
    pos = jnp.arange(s, dtype=F32)
    theta = 1.0 / (ROPE_BASE ** jnp.linspace(0.0, 1.0, RET_DK // 2, dtype=F32))
    ang = pos[:, None] * theta[None, :]
    return jnp.cos(ang), jnp.sin(ang)


def _trunk(x, layers, g_final, cos, sin):
    b, s, _ = x.shape
    tm_a = min(TM_QKV, s)
    tm_b = min(TM_MIX, s)
    tm_c = min(TM_MLP, b * s)
    assert s % tm_a == 0 and s % tm_b == 0 and (b * s) % tm_c == 0 and tm_a % CHUNK == 0 and tm_b % CHUNK == 0
    n_layers = len(layers)
    for li, p in enumerate(layers):
        q, k, v, cb = _qkv_bwd_call(x, p["dec_bwd"], p["norm_mix"], p["w_qkv"], cos, sin, tm=tm_a)
        x = _mix_call(x, q, k, v, cb, p["dec_fwd"], p["dec_bwd"], p["norm_mix"], p["w_rest"], p["conv_w"],
                      p["w_conv_out"], p["gn_w"], p["gn_b"], p["w_ret_out"], p["gate_b"], p["w_mix_out"], tm=tm_b)
        x = _mlp_call(x.reshape(b * s, D_MODEL), p["norm_mlp"], p["w_mlp_in"], p["w_mlp_out"], g_final,
                      tm=tm_c, final_norm=(li == n_layers - 1)).reshape(b, s, D_MODEL)
    return x


def kernel(x_prompt, x_sample, norm_mix, w_in, conv_w, w_conv_out, ret_decay_fwd, ret_decay_bwd, ret_gn_w,
           ret_gn_b, w_ret_out, gate_b, w_mix_out, norm_mlp, w_mlp_in, w_mlp_out, norm_final):
    depth = w_in.shape[0]
    qkv_lo, qkv_hi = 3 * CONV_W, 3 * CONV_W + 2 * RET_QK + RET_V
    layers = []
    for l in range(depth):
        w = w_in[l].astype(BF16)
        layers.append(dict(
            norm_mix=norm_mix[l].reshape(1, D_MODEL),
            w_qkv=w[:, qkv_lo:qkv_hi],
            w_rest=jnp.concatenate([w[:, :qkv_lo], w[:, qkv_hi:]], axis=1),
            conv_w=conv_w[l],
            w_conv_out=w_conv_out[l].astype(BF16),
            dec_fwd=ret_decay_fwd[l].astype(F32),
            dec_bwd=ret_decay_bwd[l].astype(F32),
            gn_w=ret_gn_w[l].reshape(1, RET_V),
            gn_b=ret_gn_b[l].reshape(1, RET_V),
            w_ret_out=w_ret_out[l].astype(BF16),
            gate_b=gate_b[l].reshape(1, 2 * D_MODEL),
            w_mix_out=w_mix_out[l].astype(BF16),
            norm_mlp=norm_mlp[l].reshape(1, D_MODEL),
            w_mlp_in=w_mlp_in[l].astype(BF16),
            w_mlp_out=w_mlp_out[l].astype(BF16),
        ))
    g_final = norm_final.reshape(1, D_MODEL)
    s_max = max(x_prompt.shape[1], x_sample.shape[1])
    cos, sin = _rope_tables(s_max)
    outs = []
    for x in (x_prompt, x_sample):
        s = x.shape[1]
        outs.append(_trunk(x, layers, g_final, cos[:s], sin[:s]))
    return tuple(outs)
```

```python
import functools

import jax
import jax.numpy as jnp
from jax import lax
from jax.experimental import pallas as pl
from jax.experimental.pallas import tpu as pltpu

D_MODEL = 1024
CONV_W = D_MODEL
RET_HEADS = 4
RET_DK = 256
RET_DV = 512
RET_QK = RET_HEADS * RET_DK
RET_V = RET_HEADS * RET_DV
D_FF = 4 * D_MODEL
NORM_EPS = 1e-6
GN_EPS = 1e-5
ROPE_BASE = 10000.0

MXU_DIM = 256
RCHUNK = MXU_DIM
HALO = 16
TM_QKV = 512
TM_MIX = 256
TM_MLP = 1024
VMEM_LIMIT_BYTES = 58 * 1024 * 1024

F32 = jnp.float32
BF16 = jnp.bfloat16


def _rmsnorm(x, g):
    ms = jnp.mean(x * x, axis=-1, keepdims=True)
    return x * lax.rsqrt(ms + NORM_EPS) * g


def _log_gamma_rows(decay_scalar, rows):
    d = jnp.full((8, 128), decay_scalar, dtype=F32)
    lg = jnp.log1p(-jnp.exp(d))
    return jnp.broadcast_to(lg[0:1, :], (rows, 128))


def _dot(a, b):
    return jnp.dot(a, b, preferred_element_type=F32)


def _dot_tn(a, b):
    return lax.dot_general(a, b, (((0,), (0,)), ((), ())), preferred_element_type=F32)


def _dot_nt(a, b):
    return lax.dot_general(a, b, (((1,), (1,)), ((), ())), preferred_element_type=F32)


def _lanes2(t):
    return jnp.concatenate([t, t], axis=-1)


def _is_first_step():
    return (pl.program_id(0) == 0) & (pl.program_id(1) == 0)


def _qkv_bwd_kernel(dec_ref, x_ref, g_ref, w_ref, cos_ref, sin_ref,
                    q_ref, k_ref, v_ref, cb_ref,
                    proj_sc, state_sc, qdec_sc, kdec_sc, cdec_sc, *, tm):
    @pl.when(_is_first_step())
    def _():
        row = lax.broadcasted_iota(jnp.int32, (RCHUNK, 128), 0).astype(F32)
        for hd in range(RET_HEADS):
            lg = _log_gamma_rows(dec_ref[hd], RCHUNK)
            qdec_sc[hd] = jnp.exp((RCHUNK - row) * lg)
            kdec_sc[hd] = jnp.exp(row * lg)
            cdec_sc[hd] = jnp.exp(RCHUNK * lg[0:8, :])

    @pl.when(pl.program_id(1) == 0)
    def _():
        state_sc[...] = jnp.zeros_like(state_sc)

    h = _rmsnorm(x_ref[0], g_ref[...]).astype(BF16)
    proj_sc[...] = _dot(h, w_ref[...])

    k_scale = RET_DK ** -0.5
    for c in reversed(range(tm // RCHUNK)):
        rows = pl.ds(c * RCHUNK, RCHUNK)
        cos = cos_ref[rows, :]
        sin = sin_ref[rows, :]
        for hd in range(RET_HEADS):
            q0 = hd * RET_DK
            k0 = RET_QK + hd * RET_DK
            v0 = 2 * RET_QK + hd * RET_DV
            q1 = proj_sc[rows, q0:q0 + 128]
            q2 = proj_sc[rows, q0 + 128:q0 + 256]
            k1 = proj_sc[rows, k0:k0 + 128]
            k2 = proj_sc[rows, k0 + 128:k0 + 256]
            qr = jnp.concatenate([q1 * cos - q2 * sin, q1 * sin + q2 * cos], axis=-1)
            kr = jnp.concatenate([k1 * cos - k2 * sin, k1 * sin + k2 * cos], axis=-1) * k_scale
            vb = proj_sc[rows, v0:v0 + RET_DV].astype(BF16)
            q_ref[0, rows, q0:q0 + RET_DK] = qr.astype(BF16)
            k_ref[0, rows, q0:q0 + RET_DK] = kr.astype(BF16)
            v_ref[0, rows, hd * RET_DV:(hd + 1) * RET_DV] = vb

            qd = (qr * _lanes2(qdec_sc[hd])).astype(BF16)
            kd = (kr * _lanes2(kdec_sc[hd])).astype(BF16)
            state = state_sc[hd]
            cross = _dot(qd, state.astype(BF16))
            cb_ref[0, rows, hd * RET_DV:(hd + 1) * RET_DV] = cross.astype(BF16)
            state_sc[hd] = state * cdec_sc[hd][0:1, 0:1] + _dot_tn(kd, vb)


def _qkv_bwd_call(x, dec_bwd, g, w_qkv, cos, sin, *, tm):
    b, s, _ = x.shape
    nt = s // tm
    rev = lambda j: nt - 1 - j
    kern = functools.partial(_qkv_bwd_kernel, tm=tm)
    const = lambda shape: pl.BlockSpec(shape, lambda i, j: (0,) * len(shape), pipeline_mode=pl.Buffered(1))
    return pl.pallas_call(
        kern,
        grid=(b, nt),
        in_specs=[
            pl.BlockSpec(memory_space=pltpu.SMEM),
            pl.BlockSpec((1, tm, D_MODEL), lambda i, j: (i, rev(j), 0)),
            const((1, D_MODEL)),
            const((D_MODEL, 2 * RET_QK + RET_V)),
            pl.BlockSpec((tm, RET_DK // 2), lambda i, j: (rev(j), 0)),
            pl.BlockSpec((tm, RET_DK // 2), lambda i, j: (rev(j), 0)),
        ],
        out_specs=[
            pl.BlockSpec((1, tm, RET_QK), lambda i, j: (i, rev(j), 0)),
            pl.BlockSpec((1, tm, RET_QK), lambda i, j: (i, rev(j), 0)),
            pl.BlockSpec((1, tm, RET_V), lambda i, j: (i, rev(j), 0)),
            pl.BlockSpec((1, tm, RET_V), lambda i, j: (i, rev(j), 0)),
        ],
        out_shape=[
            jax.ShapeDtypeStruct((b, s, RET_QK), BF16),
            jax.ShapeDtypeStruct((b, s, RET_QK), BF16),
            jax.ShapeDtypeStruct((b, s, RET_V), BF16),
            jax.ShapeDtypeStruct((b, s, RET_V), BF16),
        ],
        scratch_shapes=[
            pltpu.VMEM((tm, 2 * RET_QK + RET_V), F32),
            pltpu.VMEM((RET_HEADS, RET_DK, RET_DV), F32),
            pltpu.VMEM((RET_HEADS, RCHUNK, 128), F32),
            pltpu.VMEM((RET_HEADS, RCHUNK, 128), F32),
            pltpu.VMEM((RET_HEADS, 8, 128), F32),
        ],
        compiler_params=pltpu.CompilerParams(
            dimension_semantics=("arbitrary", "arbitrary"), vmem_limit_bytes=VMEM_LIMIT_BYTES),
        name="qkv_bwd",
    )(dec_bwd, x, g, w_qkv, cos, sin)


def _mix_kernel(decf_ref, decb_ref, x_ref, xp_ref, xn_ref, q_ref, k_ref, v_ref, cb_ref,
                g_ref, wc_ref, wg_ref, cw_ref, wco_ref, gnw_ref, gnb_ref, wro_ref, gb_ref, wmo_ref,
                o_ref,
                z_sc, ret_sc, state_sc, dmask_sc, qdec_sc, kdec_sc, cdec_sc, *, tm):
    j = pl.program_id(1)
    last = pl.num_programs(1) - 1

    @pl.when(_is_first_step())
    def _():
        row = lax.broadcasted_iota(jnp.int32, (RCHUNK, 128), 0).astype(F32)
        n = lax.broadcasted_iota(jnp.int32, (RCHUNK, 128), 0)
        for hd in range(RET_HEADS):
            lgf = _log_gamma_rows(decf_ref[hd], RCHUNK)
            lgb = _log_gamma_rows(decb_ref[hd], RCHUNK)
            qdec_sc[hd] = jnp.exp((row + 1.0) * lgf)
            kdec_sc[hd] = jnp.exp((RCHUNK - 1.0 - row) * lgf)
            cdec_sc[hd] = jnp.exp(RCHUNK * lgf[0:8, :])
            for half in range(RCHUNK // 128):
                m = lax.broadcasted_iota(jnp.int32, (RCHUNK, 128), 1) + half * 128
                diff = (n - m).astype(F32)
                dmask_sc[hd, :, half * 128:(half + 1) * 128] = jnp.where(
                    diff >= 0, jnp.exp(jnp.maximum(diff, 0.0) * lgf), jnp.exp(jnp.maximum(-diff, 0.0) * lgb))

    @pl.when(j == 0)
    def _():
        state_sc[...] = jnp.zeros_like(state_sc)

    n_chunks = tm // RCHUNK
    probs = {}
    for c in range(n_chunks):
        rows = pl.ds(c * RCHUNK, RCHUNK)
        for hd in range(RET_HEADS):
            qb = q_ref[0, rows, hd * RET_DK:(hd + 1) * RET_DK]
            kb = k_ref[0, rows, hd * RET_DK:(hd + 1) * RET_DK]
            probs[c, hd] = (_dot_nt(qb, kb) * dmask_sc[hd]).astype(BF16)

    x = x_ref[0]
    g = g_ref[...]
    h = _rmsnorm(x, g).astype(BF16)
    hp = _rmsnorm(xp_ref[0], g).astype(BF16)
    hn = _rmsnorm(xn_ref[0], g).astype(BF16)
    h_ext = jnp.concatenate([hp, h, hn], axis=0)

    z = _dot(h_ext, wc_ref[:, CONV_W:2 * CONV_W]) * _dot(h_ext, wc_ref[:, 2 * CONV_W:3 * CONV_W])
    r = lax.broadcasted_iota(jnp.int32, (tm + 2 * HALO, 1), 0)
    lo = jnp.where(j == 0, HALO, 0)
    hi = jnp.where(j == last, tm + HALO, tm + 2 * HALO)
    z_sc[...] = jnp.where((r >= lo) & (r < hi), z, 0.0)

    for c in range(n_chunks):
        rows = pl.ds(c * RCHUNK, RCHUNK)
        for hd in range(RET_HEADS):
            qb = q_ref[0, rows, hd * RET_DK:(hd + 1) * RET_DK]
            kb = k_ref[0, rows, hd * RET_DK:(hd + 1) * RET_DK]
            vb = v_ref[0, rows, hd * RET_DV:(hd + 1) * RET_DV]
            qd = (qb.astype(F32) * _lanes2(qdec_sc[hd])).astype(BF16)
            kd = (kb.astype(F32) * _lanes2(kdec_sc[hd])).astype(BF16)
            state = state_sc[hd]
            o = (_dot(probs[c, hd], vb) + _dot(qd, state.astype(BF16))
                 + cb_ref[0, rows, hd * RET_DV:(hd + 1) * RET_DV].astype(F32))
            state_sc[hd] = state * cdec_sc[hd][0:1, 0:1] + _dot_tn(kd, vb)
            mu = jnp.mean(o, axis=-1, keepdims=True)
            d = o - mu
            var = jnp.mean(d * d, axis=-1, keepdims=True)
            ret_sc[rows, hd * RET_DV:(hd + 1) * RET_DV] = d * lax.rsqrt(var + GN_EPS)

    zc = (cw_ref[0:1, :] * z_sc[pl.ds(HALO - 1, tm), :]
          + cw_ref[1:2, :] * z_sc[pl.ds(HALO, tm), :]
          + cw_ref[2:3, :] * z_sc[pl.ds(HALO + 1, tm), :])
    u = (_dot(h, wc_ref[:, 0:CONV_W]) * zc).astype(BF16)
    y_conv = _dot(u, wco_ref[...])

    o_n = ret_sc[...] * gnw_ref[...] + gnb_ref[...]
    gr = _dot(h, wg_ref[:, 0:RET_V])
    y_ret = _dot((gr * jax.nn.sigmoid(gr) * o_n).astype(BF16), wro_ref[...])

    gates = jax.nn.sigmoid(_dot(h, wg_ref[:, RET_V:]) + gb_ref[...])
    merged = (gates[:, :D_MODEL] * y_conv + gates[:, D_MODEL:] * y_ret).astype(BF16)
    o_ref[0] = x + _dot(merged, wmo_ref[...])


def _mix_call(x, q, k, v, cb, dec_fwd, dec_bwd, g, w_conv3, w_gates, conv_w, w_conv_out, gn_w, gn_b,
              w_ret_out, gate_b, w_mix_out, *, tm):
    b, s, _ = x.shape
    nt = s // tm
    hb = tm // HALO
    n_hb = s // HALO
    kern = functools.partial(_mix_kernel, tm=tm)
    const = lambda shape: pl.BlockSpec(shape, lambda i, j: (0,) * len(shape), pipeline_mode=pl.Buffered(1))
    tile = lambda width: pl.BlockSpec((1, tm, width), lambda i, j: (i, j, 0))
    smem = pl.BlockSpec(memory_space=pltpu.SMEM)
    return pl.pallas_call(
        kern,
        grid=(b, nt),
        in_specs=[
            smem, smem,
            tile(D_MODEL),
            pl.BlockSpec((1, HALO, D_MODEL), lambda i, j: (i, jnp.maximum(j * hb - 1, 0), 0)),
            pl.BlockSpec((1, HALO, D_MODEL), lambda i, j: (i, jnp.minimum((j + 1) * hb, n_hb - 1), 0)),
            tile(RET_QK), tile(RET_QK), tile(RET_V), tile(RET_V),
            const((1, D_MODEL)),
            const((D_MODEL, 3 * CONV_W)),
            const((D_MODEL, RET_V + 2 * D_MODEL)),
            const((3, CONV_W)),
            const((CONV_W, D_MODEL)),
            const((1, RET_V)), const((1, RET_V)),
            const((RET_V, D_MODEL)),
            const((1, 2 * D_MODEL)),
            const((D_MODEL, D_MODEL)),
        ],
        out_specs=tile(D_MODEL),
        out_shape=jax.ShapeDtypeStruct((b, s, D_MODEL), F32),
        scratch_shapes=[
            pltpu.VMEM((tm + 2 * HALO, CONV_W), F32),
            pltpu.VMEM((tm, RET_V), F32),
            pltpu.VMEM((RET_HEADS, RET_DK, RET_DV), F32),
            pltpu.VMEM((RET_HEADS, RCHUNK, RCHUNK), F32),
            pltpu.VMEM((RET_HEADS, RCHUNK, 128), F32),
            pltpu.VMEM((RET_HEADS, RCHUNK, 128), F32),
            pltpu.VMEM((RET_HEADS, 8, 128), F32),
        ],
        compiler_params=pltpu.CompilerParams(
            dimension_semantics=("arbitrary", "arbitrary"), vmem_limit_bytes=VMEM_LIMIT_BYTES),
        name="mix",
    )(dec_fwd, dec_bwd, x, x, x, q, k, v, cb, g, w_conv3, w_gates, conv_w, w_conv_out, gn_w, gn_b,
      w_ret_out, gate_b, w_mix_out)


def _mlp_kernel(x_ref, g_ref, w1_ref, w2_ref, gf_ref, o_ref, *, final_norm):
    x = x_ref[...]
    h = _rmsnorm(x, g_ref[...]).astype(BF16)
    acc = x
    for c in range(D_FF // D_MODEL):
        cols = slice(c * D_MODEL, (c + 1) * D_MODEL)
        a = jnp.maximum(_dot(h, w1_ref[:, cols]), 0.0)
        acc = acc + _dot((a * a).astype(BF16), w2_ref[cols, :])
    if final_norm:
        acc = _rmsnorm(acc, gf_ref[...])
    o_ref[...] = acc


def _mlp_call(x2d, g, w1, w2, g_final, *, tm, final_norm):
    t = x2d.shape[0]
    kern = functools.partial(_mlp_kernel, final_norm=final_norm)
    const = lambda shape: pl.BlockSpec(shape, lambda i: (0,) * len(shape), pipeline_mode=pl.Buffered(1))
    return pl.pallas_call(
        kern,
        grid=(t // tm,),
        in_specs=[
            pl.BlockSpec((tm, D_MODEL), lambda i: (i, 0)),
            const((1, D_MODEL)),
            const((D_MODEL, D_FF)),
            const((D_FF, D_MODEL)),
            const((1, D_MODEL)),
        ],
        out_specs=pl.BlockSpec((tm, D_MODEL), lambda i: (i, 0)),
        out_shape=jax.ShapeDtypeStruct((t, D_MODEL), F32),
        compiler_params=pltpu.CompilerParams(
            dimension_semantics=("arbitrary",), vmem_limit_bytes=VMEM_LIMIT_BYTES),
        name="mlp",
    )(x2d, g, w1, w2, g_final)


def _rope_tables(s):
    pos = jnp.arange(s, dtype=F32)
    theta = 1.0 / (ROPE_BASE ** jnp.linspace(0.0, 1.0, RET_DK // 2, dtype=F32))
    ang = pos[:, None] * theta[None, :]
    return jnp.cos(ang), jnp.sin(ang)


def _fit_tile(pref, n, quantum):
    t = min(pref, n) // quantum * quantum
    while n % t:
        t -= quantum
    return t


def _trunk(x, layers, g_final, cos, sin):
    b, s, _ = x.shape
    assert s % RCHUNK == 0
    tm_a = _fit_tile(TM_QKV, s, RCHUNK)
    tm_b = _fit_tile(TM_MIX, s, RCHUNK)
    tm_c = _fit_tile(TM_MLP, b * s, RCHUNK)
    n_layers = len(layers)
    for li, p in enumerate(layers):
        q, k, v, cb = _qkv_bwd_call(x, p["dec_bwd"], p["norm_mix"], p["w_qkv"], cos, sin, tm=tm_a)
        x = _mix_call(x, q, k, v, cb, p["dec_fwd"], p["dec_bwd"], p["norm_mix"], p["w_conv3"], p["w_gates"],
                      p["conv_w"], p["w_conv_out"], p["gn_w"], p["gn_b"], p["w_ret_out"], p["gate_b"],
                      p["w_mix_out"], tm=tm_b)
        x = _mlp_call(x.reshape(b * s, D_MODEL), p["norm_mlp"], p["w_mlp_in"], p["w_mlp_out"], g_final,
                      tm=tm_c, final_norm=(li == n_layers - 1)).reshape(b, s, D_MODEL)
    return x


def kernel(x_prompt, x_sample, norm_mix, w_in, conv_w, w_conv_out, ret_decay_fwd, ret_decay_bwd, ret_gn_w,
           ret_gn_b, w_ret_out, gate_b, w_mix_out, norm_mlp, w_mlp_in, w_mlp_out, norm_final):
    depth = w_in.shape[0]
    qkv_lo, qkv_hi = 3 * CONV_W, 3 * CONV_W + 2 * RET_QK + RET_V
    layers = []
    for l in range(depth):
        layers.append(dict(
            norm_mix=norm_mix[l].reshape(1, D_MODEL),
            w_conv3=w_in[l, :, :qkv_lo].astype(BF16),
            w_qkv=w_in[l, :, qkv_lo:qkv_hi].astype(BF16),
            w_gates=w_in[l, :, qkv_hi:].astype(BF16),
            conv_w=conv_w[l],
            w_conv_out=w_conv_out[l].astype(BF16),
            dec_fwd=ret_decay_fwd[l].astype(F32),
            dec_bwd=ret_decay_bwd[l].astype(F32),
            gn_w=ret_gn_w[l].reshape(1, RET_V),
            gn_b=ret_gn_b[l].reshape(1, RET_V),
            w_ret_out=w_ret_out[l].astype(BF16),
            gate_b=gate_b[l].reshape(1, 2 * D_MODEL),
            w_mix_out=w_mix_out[l].astype(BF16),
            norm_mlp=norm_mlp[l].reshape(1, D_MODEL),
            w_mlp_in=w_mlp_in[l].astype(BF16),
            w_mlp_out=w_mlp_out[l].astype(BF16),
        ))
    g_final = norm_final.reshape(1, D_MODEL)
    s_max = max(x_prompt.shape[1], x_sample.shape[1])
    cos, sin = _rope_tables(s_max)
    outs = []
    for x in (x_prompt, x_sample):
        s = x.shape[1]
        outs.append(_trunk(x, layers, g_final, cos[:s], sin[:s]))
    return tuple(outs)
```
